```python
import jax, jax.numpy as jnp
from jax import lax
import numpy as np

D_MODEL = 1024
BATCH = 4
SEQ = 8192
DEPTH = 4

W_MIX = 1024
N_BRANCH = 3
W_A = W_MIX
H_A = 4
DH_A = W_A // H_A
MLSTM_CHUNK = 64
QK_CONV = 4
W_B = W_MIX
CONF_K = 31
W_C = W_MIX
H_C = 8
DH_C = W_C // H_C
G_C = 2
HPG_C = H_C // G_C
KV_W = G_C * DH_C
CMP_LEN = 32
CMP_STRIDE = 16
CMP_HIDDEN = 256
SLC_LEN = 64
N_SLC = 16
WINDOW = 512
NSA_QBLOCK = 64
NEG = -1e30

IN_SPLITS = (W_A, W_A, W_A, W_A, W_A, H_A, H_A,
             W_B, W_B, W_B,
             W_C, KV_W, KV_W, KV_W, KV_W, KV_W, KV_W, 3 * H_C, W_C,
             N_BRANCH * D_MODEL)
N_IN = sum(IN_SPLITS)
F_GATE_OFFSET = 5 * W_A + H_A

kernel_name = 'hybrid_mlstm_conformer_nsa_trunk'


def split_cols(t):
    return jnp.split(t, np.cumsum(IN_SPLITS)[:-1].tolist(), axis=-1)


def rmsnorm(x, g, eps=1e-6):
    xf = x.astype(jnp.float32)
    y = xf * lax.rsqrt(jnp.mean(xf * xf, axis=-1, keepdims=True) + eps)
    return y.astype(x.dtype) * g


def layernorm(x, g, b, eps=1e-5):
    xf = x.astype(jnp.float32)
    mu = jnp.mean(xf, axis=-1, keepdims=True)
    var = jnp.mean(jnp.square(xf - mu), axis=-1, keepdims=True)
    y = (xf - mu) * lax.rsqrt(var + eps)
    return y.astype(x.dtype) * g + b


def causal_dwconv(x, w, b):
    k = w.shape[0]
    y = lax.conv_general_dilated(x, w[:, None, :].astype(x.dtype), (1,), [(k - 1, 0)],
                                 dimension_numbers=('NWC', 'WIO', 'NWC'),
                                 feature_group_count=x.shape[-1])
    return y + b


def masked_softmax(s, mask):
    s = jnp.where(mask, s, NEG)
    p = jnp.exp(s - jnp.max(s, axis=-1, keepdims=True)) * mask
    return p / jnp.maximum(jnp.sum(p, axis=-1, keepdims=True), 1e-30)


def mlstm_chunkwise(q, k, v, ig, lf):
    b, s, h, dk = q.shape
    dv = v.shape[-1]
    nc = s // MLSTM_CHUNK

    def chunks(t):
        t = t.astype(jnp.float32).reshape((b, nc, MLSTM_CHUNK, h) + t.shape[3:])
        return jnp.moveaxis(jnp.moveaxis(t, 1, 0), 3, 2)

    causal = jnp.tril(jnp.ones((MLSTM_CHUNK, MLSTM_CHUNK), bool))

    def step(carry, inp):
        C, n, m = carry
        qc, kc, vc, ic, fc = inp
        bcum = jnp.cumsum(fc, axis=-1)
        dmat = jnp.where(causal, bcum[..., :, None] - bcum[..., None, :] + ic[..., None, :], -jnp.inf)
        m_inter = bcum + m[..., None]
        m_t = jnp.maximum(m_inter, jnp.max(dmat, axis=-1))
        w = jnp.exp(dmat - m_t[..., None])
        a_inter = jnp.exp(m_inter - m_t)
        sqk = jnp.einsum('bhtd,bhsd->bhts', qc, kc) * w
        num = jnp.einsum('bhts,bhse->bhte', sqk, vc) + a_inter[..., None] * jnp.einsum('bhtd,bhde->bhte', qc, C)
        den = jnp.sum(sqk, axis=-1) + a_inter * jnp.einsum('bhtd,bhd->bht', qc, n)
        h_out = num / jnp.maximum(jnp.abs(den), jnp.exp(-m_t))[..., None]
        b_last = bcum[..., -1]
        g = b_last[..., None] - bcum + ic
        m_new = jnp.maximum(b_last + m, jnp.max(g, axis=-1))
        decay = jnp.exp(b_last + m - m_new)
        wk = jnp.exp(g - m_new[..., None])[..., None] * kc
        C = decay[..., None, None] * C + jnp.einsum('bhsd,bhse->bhde', wk, vc)
        n = decay[..., None] * n + jnp.sum(wk, axis=2)
        return (C, n, m_new), h_out

    init = (jnp.zeros((b, h, dk, dv), jnp.float32), jnp.zeros((b, h, dk), jnp.float32),
            jnp.zeros((b, h), jnp.float32))
    _, hs = lax.scan(step, init, (chunks(q), chunks(k), chunks(v), chunks(ig), chunks(lf)))
    return hs.transpose(1, 0, 3, 2, 4).reshape(b, s, h, dv).astype(q.dtype)


def nsa(q, k_cmp, v_cmp, k_slc, v_slc, k_win, v_win, gates, pe, w1, w2):
    b, s = q.shape[:2]
    scale = DH_C ** -0.5

    def compress(t, j):
        tb = t.reshape(b, s // CMP_STRIDE, CMP_STRIDE, G_C, DH_C)
        blk = jnp.concatenate([tb[:, :-1], tb[:, 1:]], axis=2) + pe[j][None, None, :, None, :]
        blk = blk.transpose(0, 1, 3, 2, 4).reshape(b, -1, G_C, CMP_LEN * DH_C)
        return jax.nn.gelu(blk @ w1[j]) @ w2[j]

    kc = compress(k_cmp, 0)
    vc = compress(v_cmp, 1)
    n_cmp = kc.shape[1]
    cmp_start = jnp.arange(n_cmp) * CMP_STRIDE
    cmp_end = cmp_start + CMP_LEN - 1
    n_slc = s // SLC_LEN
    n_sel = min(N_SLC, n_slc)
    slc_start = jnp.arange(n_slc) * SLC_LEN
    overlap = ((cmp_start[:, None] < slc_start[None, :] + SLC_LEN) &
               (cmp_start[:, None] + CMP_LEN > slc_start[None, :])).astype(jnp.float32)
    ks_blocks = k_slc.reshape(b, n_slc, SLC_LEN, G_C, DH_C).transpose(0, 3, 1, 2, 4)
    vs_blocks = v_slc.reshape(b, n_slc, SLC_LEN, G_C, DH_C).transpose(0, 3, 1, 2, 4)
    pad = ((0, 0), (WINDOW, 0), (0, 0), (0, 0))
    kw_pad = jnp.pad(k_win, pad)
    vw_pad = jnp.pad(v_win, pad)
    gather = jax.vmap(jax.vmap(lambda blocks, idx: blocks[idx]))
    jb = jnp.arange(n_slc)

    def block(i):
        qs = i * NSA_QBLOCK
        t = qs + jnp.arange(NSA_QBLOCK)
        qb = lax.dynamic_slice_in_dim(q, qs, NSA_QBLOCK, axis=1).reshape(b, NSA_QBLOCK, G_C, HPG_C, DH_C)
        gb = jax.nn.sigmoid(lax.dynamic_slice_in_dim(gates, qs, NSA_QBLOCK, axis=1)
                            .reshape(b, NSA_QBLOCK, G_C, HPG_C, 3))
        s_c = jnp.einsum('bqgjd,bngd->bgjqn', qb, kc).astype(jnp.float32) * scale
        p_c = masked_softmax(s_c, cmp_end[None, :] <= t[:, None])
        o_c = jnp.einsum('bgjqn,bngd->bqgjd', p_c.astype(vc.dtype), vc)
        imp = jnp.einsum('bgjqn,nm->bgqm', p_c, overlap)
        cur = t // SLC_LEN
        forced = (jb[None] == 0) | (jb[None] == cur[:, None]) | (jb[None] == cur[:, None] - 1)
        imp = jnp.where(jb[None] > cur[:, None], -1e4, jnp.where(forced, 1e4, imp))
        _, idx = lax.top_k(imp, n_sel)
        ks = gather(ks_blocks, idx).reshape(b, G_C, NSA_QBLOCK, n_sel * SLC_LEN, DH_C)
        vs = gather(vs_blocks, idx).reshape(b, G_C, NSA_QBLOCK, n_sel * SLC_LEN, DH_C)
        pos = (idx[..., None] * SLC_LEN + jnp.arange(SLC_LEN)).reshape(b, G_C, NSA_QBLOCK, n_sel * SLC_LEN)
        s_s = jnp.einsum('bqgjd,bgqkd->bgjqk', qb, ks).astype(jnp.float32) * scale
        p_s = masked_softmax(s_s, (pos <= t[:, None])[:, :, None])
        o_s = jnp.einsum('bgjqk,bgqkd->bqgjd', p_s.astype(vs.dtype), vs)
        kw = lax.dynamic_slice_in_dim(kw_pad, qs, NSA_QBLOCK + WINDOW, axis=1)
        vw = lax.dynamic_slice_in_dim(vw_pad, qs, NSA_QBLOCK + WINDOW, axis=1)
        pw = qs - WINDOW + jnp.arange(NSA_QBLOCK + WINDOW)
        dlt = t[:, None] - pw[None, :]
        mask_w = (dlt >= 0) & (dlt < WINDOW) & (pw[None, :] >= 0)
        s_w = jnp.einsum('bqgjd,bkgd->bgjqk', qb, kw).astype(jnp.float32) * scale
        p_w = masked_softmax(s_w, mask_w)
        o_w = jnp.einsum('bgjqk,bkgd->bqgjd', p_w.astype(vw.dtype), vw)
        return gb[..., 0:1] * o_c + gb[..., 1:2] * o_s + gb[..., 2:3] * o_w

    out = lax.map(block, jnp.arange(s // NSA_QBLOCK))
    return out.transpose(1, 0, 2, 3, 4, 5).reshape(b, s, W_C)


def setup_inputs(seed: int = 0) -> dict:
    key = jax.random.key(seed)
    ks = jax.random.split(key, 20)
    L = DEPTH

    def nrm(k, shape, sc):
        return jax.random.normal(k, shape, jnp.float32) * sc

    b_in = nrm(ks[7], (L, N_IN), 0.02)
    b_in = b_in.at[:, F_GATE_OFFSET:F_GATE_OFFSET + H_A].add(jnp.linspace(3.0, 6.0, H_A))
    return {
        'x': nrm(ks[0], (BATCH, SEQ, D_MODEL), 1.0),
        'c': nrm(ks[1], (BATCH, D_MODEL), 1.0),
        'w_ada': nrm(ks[2], (L, D_MODEL, 3 * D_MODEL), 0.3 * D_MODEL ** -0.5),
        'b_ada': nrm(ks[3], (L, 3 * D_MODEL), 0.02),
        'norm_pre': 1.0 + nrm(ks[4], (L, D_MODEL), 0.02),
        'norm_post': 1.0 + nrm(ks[5], (L, D_MODEL), 0.02),
        'w_in': nrm(ks[6], (L, D_MODEL, N_IN), D_MODEL ** -0.5),
        'b_in': b_in,
        'mlstm_conv_w': nrm(ks[8], (L, QK_CONV, 2 * W_A), QK_CONV ** -0.5),
        'mlstm_conv_b': nrm(ks[9], (L, 2 * W_A), 0.02),
        'mlstm_norm': 1.0 + nrm(ks[10], (L, W_A), 0.02),
        'conf_dw_w': nrm(ks[11], (L, CONF_K, W_B), CONF_K ** -0.5),
        'conf_dw_b': nrm(ks[12], (L, W_B), 0.02),
        'conf_ln_g': 1.0 + nrm(ks[13], (L, W_B), 0.02),
        'conf_ln_b': nrm(ks[14], (L, W_B), 0.02),
        'nsa_cmp_pe': nrm(ks[15], (L, 2, CMP_LEN, DH_C), 0.02),
        'nsa_cmp_w1': nrm(ks[16], (L, 2, CMP_LEN * DH_C, CMP_HIDDEN), (CMP_LEN * DH_C) ** -0.5),
        'nsa_cmp_w2': nrm(ks[17], (L, 2, CMP_HIDDEN, DH_C), CMP_HIDDEN ** -0.5),
        'w_branch': nrm(ks[18], (L, N_BRANCH, W_MIX, D_MODEL), W_MIX ** -0.5),
        'w_out': nrm(ks[19], (L, D_MODEL, D_MODEL), D_MODEL ** -0.5),
    }


def reference(x, c, w_ada, b_ada, norm_pre, norm_post, w_in, b_in, mlstm_conv_w, mlstm_conv_b,
              mlstm_norm, conf_dw_w, conf_dw_b, conf_ln_g, conf_ln_b, nsa_cmp_pe, nsa_cmp_w1,
              nsa_cmp_w2, w_branch, w_out):
    b, s = x.shape[:2]
    for l in range(DEPTH):
        shift, scale, gate = jnp.split(c @ w_ada[l] + b_ada[l], 3, axis=-1)
        h = rmsnorm(x, norm_pre[l]) * (1.0 + scale[:, None]) + shift[:, None]
        proj = h @ w_in[l] + b_in[l]
        (aq, ak, av, ao, az, ai, af, ba, bb, bz,
         cq, ckc, cvc, cks, cvs, ckw, cvw, cg, cz, mg) = split_cols(proj)

        qk = jax.nn.silu(causal_dwconv(jnp.concatenate([aq, ak], axis=-1), mlstm_conv_w[l], mlstm_conv_b[l]))
        mq, mk = jnp.split(qk, 2, axis=-1)
        hcell = mlstm_chunkwise(mq.reshape(b, s, H_A, DH_A) * DH_A ** -0.5, mk.reshape(b, s, H_A, DH_A),
                                av.reshape(b, s, H_A, DH_A), ai, jax.nn.log_sigmoid(af))
        hf = hcell.astype(jnp.float32)
        mu = jnp.mean(hf, axis=-1, keepdims=True)
        hf = (hf - mu) * lax.rsqrt(jnp.mean(jnp.square(hf - mu), axis=-1, keepdims=True) + 1e-5)
        hn = hf.reshape(b, s, W_A).astype(x.dtype) * mlstm_norm[l]
        y_a = jax.nn.sigmoid(ao) * hn * jax.nn.silu(az)

        u = causal_dwconv(ba * jax.nn.sigmoid(bb), conf_dw_w[l], conf_dw_b[l])
        y_b = jax.nn.silu(layernorm(u, conf_ln_g[l], conf_ln_b[l])) * jax.nn.silu(bz)

        hc = nsa(cq.reshape(b, s, H_C, DH_C),
                 ckc.reshape(b, s, G_C, DH_C), cvc.reshape(b, s, G_C, DH_C),
                 cks.reshape(b, s, G_C, DH_C), cvs.reshape(b, s, G_C, DH_C),
                 ckw.reshape(b, s, G_C, DH_C), cvw.reshape(b, s, G_C, DH_C),
                 cg.reshape(b, s, H_C, 3), nsa_cmp_pe[l], nsa_cmp_w1[l], nsa_cmp_w2[l])
        y_c = hc * jax.nn.silu(cz)

        g = jax.nn.sigmoid(mg).reshape(b, s, N_BRANCH, D_MODEL)
        merged = (g[:, :, 0] * (y_a @ w_branch[l, 0]) + g[:, :, 1] * (y_b @ w_branch[l, 1])
                  + g[:, :, 2] * (y_c @ w_branch[l, 2]))
        out = merged @ w_out[l]
        x = x + gate[:, None] * rmsnorm(out, norm_post[l])
    return x
```

```python
import functools

import jax
import jax.numpy as jnp
import numpy as np
from jax import lax
from jax.experimental import pallas as pl
from jax.experimental.pallas import tpu as pltpu

F32 = jnp.float32
BF16 = jnp.bfloat16
HIGHEST = lax.Precision.HIGHEST

D_MODEL = 1024
W_MIX = 1024
N_BRANCH = 3
H_A = 4
DH_A = W_MIX // H_A
QK_CONV = 4
CONF_K = 31
H_C = 8
DH_C = W_MIX // H_C
G_C = 2
HPG_C = H_C // G_C
KV_W = G_C * DH_C
CMP_LEN = 32
CMP_STRIDE = 16
CMP_HIDDEN = 256
SLC_LEN = 64
SLC_SHIFT = 6
N_SLC = 16
WINDOW = 512
NEG = -1e30
LANE = 128

COL_MG = 0
COL_AQ, COL_AK, COL_AV, COL_AO, COL_AZ = 3072, 4096, 5120, 6144, 7168
COL_BA, COL_BB, COL_BZ = 8192, 9216, 10240
COL_CQ = 11264
COL_CZ = 12288
COL_KC, COL_VC, COL_KS, COL_VS, COL_KW, COL_VW = 13312, 13568, 13824, 14080, 14336, 14592
COL_SM = 14848
COL_GATE = 14976
N_PROJ = COL_GATE + G_C * LANE

TN_PROJ = 2176
L_MLSTM = 256
T_CONF = 256
CONF_HALO = 32
TQ_CMP = 256
TQ_ATT = 128
TK_ATT = 512
TS_PREP = 1024
TM_OUT = 512
VMEM_LIMIT = 56 * 1024 * 1024


def _cparams(sem):
    return pltpu.CompilerParams(dimension_semantics=sem, vmem_limit_bytes=VMEM_LIMIT)


def _sigmoid(x):
    return jax.nn.sigmoid(x)


def _silu(x):
    return x * jax.nn.sigmoid(x)


def _nt_dot(a, b):
    return lax.dot_general(a, b, (((1,), (1,)), ((), ())), preferred_element_type=F32)


def _ada_kernel(c_ref, w_ref, b_ref, o_ref):
    o_ref[0] = jnp.dot(c_ref[...], w_ref[0], precision=HIGHEST, preferred_element_type=F32) + b_ref[0]


def _ada_call(c_pad, w_ada, b_ada):
    depth = w_ada.shape[0]
    return pl.pallas_call(
        _ada_kernel,
        grid=(depth, 3),
        in_specs=[pl.BlockSpec((8, D_MODEL), lambda l, j: (0, 0)),
                  pl.BlockSpec((1, D_MODEL, D_MODEL), lambda l, j: (l, 0, j)),
                  pl.BlockSpec((1, 1, D_MODEL), lambda l, j: (l, 0, j))],
        out_specs=pl.BlockSpec((1, 8, D_MODEL), lambda l, j: (l, 0, j)),
        out_shape=jax.ShapeDtypeStruct((depth, 8, 3 * D_MODEL), F32),
        compiler_params=_cparams(("arbitrary", "arbitrary")),
        name="adaln",
    )(c_pad, w_ada, b_ada.reshape(depth, 1, 3 * D_MODEL))


def _inproj_kernel(x_ref, mod_ref, g_ref, w_ref, b_ref, wt_ref, bt_ref, proj_ref, gt_ref, h_scr):
    @pl.when(pl.program_id(2) == 0)
    def _():
        x = x_ref[0]
        ms = jnp.mean(x * x, axis=-1, keepdims=True)
        y = x * lax.rsqrt(ms + 1e-6) * g_ref[...]
        h = (y * (1.0 + mod_ref[0, 1:2, :]) + mod_ref[0, 0:1, :]).astype(BF16)
        h_scr[...] = h
        gt_ref[0] = _nt_dot(wt_ref[...], h) + bt_ref[...]

    proj_ref[0] = jnp.dot(h_scr[...], w_ref[...], preferred_element_type=F32) + b_ref[...]


def _inproj_call(x, mod, g_pre, w, b, wt, bt, tm):
    bsz, s, _ = x.shape
    nj = N_PROJ // TN_PROJ
    return pl.pallas_call(
        _inproj_kernel,
        grid=(bsz, s // tm, nj),
        in_specs=[pl.BlockSpec((1, tm, D_MODEL), lambda b_, i, j: (b_, i, 0)),
                  pl.BlockSpec((1, 8, D_MODEL), lambda b_, i, j: (b_, 0, 0)),
                  pl.BlockSpec((1, D_MODEL), lambda b_, i, j: (0, 0)),
                  pl.BlockSpec((D_MODEL, TN_PROJ), lambda b_, i, j: (0, j)),
                  pl.BlockSpec((1, TN_PROJ), lambda b_, i, j: (0, j)),
                  pl.BlockSpec((8, D_MODEL), lambda b_, i, j: (0, 0)),
                  pl.BlockSpec((8, 1), lambda b_, i, j: (0, 0))],
        out_specs=[pl.BlockSpec((1, tm, TN_PROJ), lambda b_, i, j: (b_, i, j)),
                   pl.BlockSpec((1, 8, tm), lambda b_, i, j: (b_, 0, i))],
        out_shape=[jax.ShapeDtypeStruct((bsz, s, N_PROJ), F32),
                   jax.ShapeDtypeStruct((bsz, 8, s), F32)],
        scratch_shapes=[pltpu.VMEM((tm, D_MODEL), BF16)],
        compiler_params=_cparams(("arbitrary", "arbitrary", "arbitrary")),
        name="inproj",
    )(x, mod, g_pre, w, b, wt, bt)


def _log_sigmoid(x):
    return jnp.minimum(x, 0.0) - jnp.log1p(jnp.exp(-jnp.abs(x)))


def _mlstm_kernel(q_ref, k_ref, v_ref, o_ref, z_ref, sm_ref, gt_ref, cw_ref, cb_ref, nrm_ref,
                  y_ref, xq_scr, xk_scr, c_scr, n_scr, m_scr):
    L = L_MLSTM

    @pl.when(pl.program_id(1) == 0)
    def _():
        xq_scr[0:8, :] = jnp.zeros((8, W_MIX), F32)
        xk_scr[0:8, :] = jnp.zeros((8, W_MIX), F32)
        c_scr[...] = jnp.zeros_like(c_scr)
        n_scr[...] = jnp.zeros_like(n_scr)
        m_scr[...] = jnp.zeros_like(m_scr)

    xq_scr[8:8 + L, :] = q_ref[0]
    xk_scr[8:8 + L, :] = k_ref[0]
    accq = jnp.broadcast_to(cb_ref[:, 0:W_MIX], (L, W_MIX))
    acck = jnp.broadcast_to(cb_ref[:, W_MIX:2 * W_MIX], (L, W_MIX))
    for j in range(QK_CONV):
        off = 8 - (QK_CONV - 1) + j
        accq = accq + cw_ref[j:j + 1, 0:W_MIX] * xq_scr[off:off + L, :]
        acck = acck + cw_ref[j:j + 1, W_MIX:2 * W_MIX] * xk_scr[off:off + L, :]
    xq_scr[0:8, :] = xq_scr[L:L + 8, :]
    xk_scr[0:8, :] = xk_scr[L:L + 8, :]
    qc = _silu(accq) * (DH_A ** -0.5)
    kc = _silu(acck)

    sm = sm_ref[0]
    row = lax.broadcasted_iota(jnp.int32, (L, L), 0)
    col = lax.broadcasted_iota(jnp.int32, (L, L), 1)
    causal = col <= row
    tril = causal.astype(F32)
    triu = (row <= col).astype(F32)
    bcum_c = jnp.dot(tril, _log_sigmoid(sm), precision=HIGHEST, preferred_element_type=F32)
    gt = gt_ref[0]
    bcum_r = jnp.dot(_log_sigmoid(gt), triu, precision=HIGHEST, preferred_element_type=F32)

    for h in range(H_A):
        cs = slice(h * DH_A, (h + 1) * DH_A)
        qh = qc[:, cs]
        kh = kc[:, cs]
        qb = qh.astype(BF16)
        kb = kh.astype(BF16)
        vb = v_ref[0, :, cs].astype(BF16)
        i_c = sm[:, h:h + 1]
        b_c = bcum_c[:, H_A + h:H_A + h + 1]
        r_r = gt[h:h + 1, :] - bcum_r[H_A + h:H_A + h + 1, :]
        m_prev = m_scr[h][:, 0:1]
        c_prev = c_scr[h]
        n_prev = n_scr[h]

        dmat = jnp.where(causal, b_c + r_r, -jnp.inf)
        m_inter = b_c + m_prev
        m_t = jnp.maximum(m_inter, jnp.max(dmat, axis=-1, keepdims=True))
        w = jnp.exp(dmat - m_t)
        a_inter = jnp.exp(m_inter - m_t)
        sqk = _nt_dot(qb, kb) * w
        num = (jnp.dot(sqk.astype(BF16), vb, preferred_element_type=F32)
               + a_inter * jnp.dot(qb, c_prev.astype(BF16), preferred_element_type=F32))
        den = jnp.sum(sqk, axis=-1, keepdims=True) + a_inter * jnp.sum(qh * n_prev, axis=-1, keepdims=True)
        hcell = num / jnp.maximum(jnp.abs(den), jnp.exp(-m_t))

        mu = jnp.mean(hcell, axis=-1, keepdims=True)
        hc0 = hcell - mu
        hn = hc0 * lax.rsqrt(jnp.mean(hc0 * hc0, axis=-1, keepdims=True) + 1e-5) * nrm_ref[:, cs]
        y_ref[0, :, cs] = _sigmoid(o_ref[0, :, cs]) * hn * _silu(z_ref[0, :, cs])

        b_last = b_c[L - 1:L, :]
        g = b_last - b_c + i_c
        m_new = jnp.maximum(b_last + m_prev, jnp.max(g, axis=0, keepdims=True))
        decay = jnp.exp(b_last + m_prev - m_new)
        wk = jnp.exp(g - m_new) * kh
        c_scr[h] = decay * c_prev + lax.dot_general(wk.astype(BF16), vb, (((0,), (0,)), ((), ())),
                                                    preferred_element_type=F32)
        n_scr[h] = decay * n_prev + jnp.sum(wk, axis=0, keepdims=True)
        m_scr[h] = jnp.broadcast_to(m_new, (1, LANE))


def _mlstm_call(proj, gt, conv_w, conv_b, nrm):
    bsz, s, _ = proj.shape
    L = L_MLSTM
    wblk = W_MIX

    def col_spec(col):
        cb = col // wblk
        return pl.BlockSpec((1, L, wblk), lambda b_, i: (b_, i, cb))

    return pl.pallas_call(
        _mlstm_kernel,
        grid=(bsz, s // L),
        in_specs=[col_spec(COL_AQ), col_spec(COL_AK), col_spec(COL_AV), col_spec(COL_AO), col_spec(COL_AZ),
                  pl.BlockSpec((1, L, LANE), lambda b_, i: (b_, i, COL_SM // LANE)),
                  pl.BlockSpec((1, 8, L), lambda b_, i: (b_, 0, i)),
                  pl.BlockSpec((QK_CONV, 2 * W_MIX), lambda b_, i: (0, 0)),
                  pl.BlockSpec((1, 2 * W_MIX), lambda b_, i: (0, 0)),
                  pl.BlockSpec((1, W_MIX), lambda b_, i: (0, 0))],
        out_specs=pl.BlockSpec((1, L, W_MIX), lambda b_, i: (b_, i, 0)),
        out_shape=jax.ShapeDtypeStruct((bsz, s, W_MIX), F32),
        scratch_shapes=[pltpu.VMEM((L + 8, W_MIX), F32), pltpu.VMEM((L + 8, W_MIX), F32),
                        pltpu.VMEM((H_A, DH_A, DH_A), F32), pltpu.VMEM((H_A, 1, DH_A), F32),
                        pltpu.VMEM((H_A, 1, LANE), F32)],
        compiler_params=_cparams(("arbitrary", "arbitrary")),
        name="mlstm",
    )(proj, proj, proj, proj, proj, proj, gt, conv_w, conv_b, nrm)


def _conf_kernel(a_ref, b_ref, z_ref, w_ref, cb_ref, lg_ref, lb_ref, y_ref, g_scr, u_scr):
    T = T_CONF
    H = CONF_HALO

    @pl.when(pl.program_id(1) == 0)
    def _():
        g_scr[0:H, :] = jnp.zeros((H, W_MIX), F32)

    g_scr[H:H + T, :] = a_ref[0] * _sigmoid(b_ref[0])
    for c in range(W_MIX // LANE):
        cs = slice(c * LANE, (c + 1) * LANE)
        acc = jnp.broadcast_to(cb_ref[:, cs], (T, LANE))
        for j in range(CONF_K):
            off = H - (CONF_K - 1) + j
            acc = acc + w_ref[j:j + 1, cs] * g_scr[off:off + T, cs]
        u_scr[:, cs] = acc
    g_scr[0:H, :] = g_scr[T:T + H, :]

    u = u_scr[...]
    mu = jnp.mean(u, axis=-1, keepdims=True)
    uc = u - mu
    var = jnp.mean(uc * uc, axis=-1, keepdims=True)
    ln = uc * lax.rsqrt(var + 1e-5) * lg_ref[...] + lb_ref[...]
    y_ref[0] = _silu(ln) * _silu(z_ref[0])


def _conf_call(proj, dw_w, dw_b, ln_g, ln_b):
    bsz, s, _ = proj.shape
    T = T_CONF

    def col_spec(col):
        cb = col // W_MIX
        return pl.BlockSpec((1, T, W_MIX), lambda b_, i: (b_, i, cb))

    vec = pl.BlockSpec((1, W_MIX), lambda b_, i: (0, 0))
    return pl.pallas_call(
        _conf_kernel,
        grid=(bsz, s // T),
        in_specs=[col_spec(COL_BA), col_spec(COL_BB), col_spec(COL_BZ),
                  pl.BlockSpec((CONF_K, W_MIX), lambda b_, i: (0, 0)), vec, vec, vec],
        out_specs=pl.BlockSpec((1, T, W_MIX), lambda b_, i: (b_, i, 0)),
        out_shape=jax.ShapeDtypeStruct((bsz, s, W_MIX), F32),
        scratch_shapes=[pltpu.VMEM((T + CONF_HALO, W_MIX), F32), pltpu.VMEM((T, W_MIX), F32)],
        compiler_params=_cparams(("arbitrary", "arbitrary")),
        name="conformer",
    )(proj, proj, proj, dw_w, dw_b, ln_g, ln_b)


def _gelu_tanh(x):
    return 0.5 * x * (1.0 + jnp.tanh(np.sqrt(2.0 / np.pi) * (x + 0.044715 * (x * x * x))))


def _compress_kernel(x_ref, pe_ref, w1_ref, w2_ref, o_ref, b_scr):
    nb = o_ref.shape[3]
    acc_a = jnp.zeros((nb, CMP_HIDDEN), F32)
    acc_b = jnp.zeros((nb, CMP_HIDDEN), F32)
    for p in range(CMP_STRIDE):
        xp = x_ref[0, pl.ds(p, nb, stride=CMP_STRIDE), :]
        xa = (xp + pe_ref[0, p:p + 1, :]).astype(BF16)
        xb = (xp + pe_ref[0, CMP_STRIDE + p:CMP_STRIDE + p + 1, :]).astype(BF16)
        acc_a = acc_a + jnp.dot(xa, w1_ref[0, p * DH_C:(p + 1) * DH_C, :], preferred_element_type=F32)
        acc_b = acc_b + jnp.dot(xb, w1_ref[0, (CMP_STRIDE + p) * DH_C:(CMP_STRIDE + p + 1) * DH_C, :],
                                preferred_element_type=F32)
    b_scr[0:nb, :] = acc_b
    b_scr[nb:nb + 8, :] = jnp.zeros((8, CMP_HIDDEN), F32)
    hid = _gelu_tanh(acc_a + b_scr[1:nb + 1, :])
    o_ref[0, 0, 0] = jnp.dot(hid.astype(BF16), w2_ref[0], preferred_element_type=F32)


def _compress_call(proj, pe, w1, w2):
    bsz, s, _ = proj.shape
    nb = s // CMP_STRIDE
    return pl.pallas_call(
        _compress_kernel,
        grid=(bsz, 2, G_C),
        in_specs=[pl.BlockSpec((1, s, DH_C), lambda b_, j, g: (b_, 0, COL_KC // DH_C + 2 * j + g)),
                  pl.BlockSpec((1, CMP_LEN, DH_C), lambda b_, j, g: (j, 0, 0)),
                  pl.BlockSpec((1, CMP_LEN * DH_C, CMP_HIDDEN), lambda b_, j, g: (j, 0, 0)),
                  pl.BlockSpec((1, CMP_HIDDEN, DH_C), lambda b_, j, g: (j, 0, 0))],
        out_specs=pl.BlockSpec((1, 1, 1, nb, DH_C), lambda b_, j, g: (b_, j, g, 0, 0)),
        out_shape=jax.ShapeDtypeStruct((bsz, 2, G_C, nb, DH_C), F32),
        scratch_shapes=[pltpu.VMEM((nb + 8, CMP_HIDDEN), F32)],
        compiler_params=_cparams(("arbitrary", "arbitrary", "arbitrary")),
        name="nsa_compress",
    )(proj, pe, w1, w2)


def _kvprep_kernel(ks_ref, vs_ref, kw_ref, vw_ref, kp_ref, vso_ref, kwo_ref, vwo_ref):
    ts = ks_ref.shape[1]
    pos = pl.program_id(2) * ts + lax.broadcasted_iota(jnp.int32, (ts, LANE), 0)
    lane = lax.broadcasted_iota(jnp.int32, (ts, LANE), 1)
    onehot = jnp.where(lane == (pos >> SLC_SHIFT), 1.0, 0.0).astype(BF16)
    kp_ref[0, 0, :, 0:DH_C] = ks_ref[0].astype(BF16)
    kp_ref[0, 0, :, DH_C:2 * DH_C] = onehot
    vso_ref[0, 0] = vs_ref[0].astype(BF16)
    kwo_ref[0, 0] = kw_ref[0].astype(BF16)
    vwo_ref[0, 0] = vw_ref[0].astype(BF16)


def _kvprep_call(proj):
    bsz, s, _ = proj.shape
    ts = min(TS_PREP, s)

    def col_spec(col):
        cb = col // DH_C
        return pl.BlockSpec((1, ts, DH_C), lambda b_, g, i: (b_, i, cb + g))

    def out_spec(w):
        return pl.BlockSpec((1, 1, ts, w), lambda b_, g, i: (b_, g, i, 0))

    return pl.pallas_call(
        _kvprep_kernel,
        grid=(bsz, G_C, s // ts),
        in_specs=[col_spec(COL_KS), col_spec(COL_VS), col_spec(COL_KW), col_spec(COL_VW)],
        out_specs=[out_spec(2 * DH_C), out_spec(DH_C), out_spec(DH_C), out_spec(DH_C)],
        out_shape=[jax.ShapeDtypeStruct((bsz, G_C, s, 2 * DH_C), BF16)]
        + [jax.ShapeDtypeStruct((bsz, G_C, s, DH_C), BF16)] * 3,
        compiler_params=_cparams(("arbitrary", "arbitrary", "arbitrary")),
        name="nsa_kvprep",
    )(proj, proj, proj, proj)


def _cmpsel_kernel(q_ref, kc_ref, vc_ref, qa_ref, oc_ref):
    tq = q_ref.shape[1]
    nb = kc_ref.shape[3]
    t = pl.program_id(2) * tq + lax.broadcasted_iota(jnp.int32, (tq, 1), 0)
    n = lax.broadcasted_iota(jnp.int32, (1, nb), 1)
    valid = (n * CMP_STRIDE + (CMP_LEN - 1) <= t) & (n < nb - 1)
    kc = kc_ref[0, 0, 0].astype(BF16)
    vc = vc_ref[0, 0, 0].astype(BF16)
    psum = jnp.zeros((tq, nb), F32)
    for j in range(HPG_C):
        qj = (q_ref[0, :, j * DH_C:(j + 1) * DH_C] * (DH_C ** -0.5)).astype(BF16)
        qa_ref[0, :, j * 2 * DH_C:j * 2 * DH_C + DH_C] = qj
        sc = jnp.where(valid, _nt_dot(qj, kc), NEG)
        p = jnp.where(valid, jnp.exp(sc - jnp.max(sc, axis=-1, keepdims=True)), 0.0)
        p = p / jnp.maximum(jnp.sum(p, axis=-1, keepdims=True), 1e-30)
        oc_ref[0, :, j * DH_C:(j + 1) * DH_C] = jnp.dot(p.astype(BF16), vc, preferred_element_type=F32)
        psum = psum + p

    nn = lax.broadcasted_iota(jnp.int32, (nb, LANE), 0)
    mm = lax.broadcasted_iota(jnp.int32, (nb, LANE), 1)
    overlap = ((nn * CMP_STRIDE < mm * SLC_LEN + SLC_LEN) & (nn * CMP_STRIDE + CMP_LEN > mm * SLC_LEN)
               & (nn < nb - 1)).astype(F32)
    imp = jnp.dot(psum, overlap, precision=HIGHEST, preferred_element_type=F32)
    jb = lax.broadcasted_iota(jnp.int32, (tq, LANE), 1)
    cur = t >> SLC_SHIFT
    forced = (jb == 0) | (jb == cur) | (jb == cur - 1)
    imp = jnp.where(jb > cur, -1e4, jnp.where(forced, 1e4, imp))

    jf = jb.astype(F32)
    bias = jnp.full((tq, LANE), NEG, F32)
    for _ in range(N_SLC):
        mx = jnp.max(imp, axis=-1, keepdims=True)
        idx = jnp.min(jnp.where(imp == mx, jf, float(LANE)), axis=-1, keepdims=True)
        hit = jf == idx
        bias = jnp.where(hit, 0.0, bias)
        imp = jnp.where(hit, -jnp.inf, imp)
    bias = jnp.where(jb <= cur, bias, NEG).astype(BF16)
    for j in range(HPG_C):
        qa_ref[0, :, j * 2 * DH_C + DH_C:(j + 1) * 2 * DH_C] = bias


def _cmpsel_call(proj, kvc):
    bsz, s, _ = proj.shape
    nb = s // CMP_STRIDE
    tq = TQ_CMP
    gw = HPG_C * DH_C
    return pl.pallas_call(
        _cmpsel_kernel,
        grid=(bsz, G_C, s // tq),
        in_specs=[pl.BlockSpec((1, tq, gw), lambda b_, g, i: (b_, i, COL_CQ // gw + g)),
                  pl.BlockSpec((1, 1, 1, nb, DH_C), lambda b_, g, i: (b_, 0, g, 0, 0)),
                  pl.BlockSpec((1, 1, 1, nb, DH_C), lambda b_, g, i: (b_, 1, g, 0, 0))],
        out_specs=[pl.BlockSpec((1, tq, 2 * gw), lambda b_, g, i: (b_, i, g)),
                   pl.BlockSpec((1, tq, gw), lambda b_, g, i: (b_, i, g))],
        out_shape=[jax.ShapeDtypeStruct((bsz, s, 2 * W_MIX), BF16),
                   jax.ShapeDtypeStruct((bsz, s, W_MIX), F32)],
        compiler_params=_cparams(("arbitrary", "arbitrary", "arbitrary")),
        name="nsa_cmpsel",
    )(proj, kvc, kvc)


def _attn_kernel(qa_ref, kp_ref, vs_ref, kw_ref, vw_ref, oc_ref, gate_ref, z_ref, y_ref):
    tq = TQ_ATT
    tk = TK_ATT
    rows = HPG_C * tq
    qs = pl.program_id(2) * tq
    qa = qa_ref[0]
    q2 = jnp.concatenate([qa[:, j * 2 * DH_C:(j + 1) * 2 * DH_C] for j in range(HPG_C)], axis=0)
    t = qs + (lax.broadcasted_iota(jnp.int32, (rows, 1), 0) & (tq - 1))

    def flash_step(start, carry, causal):
        m, l, acc = carry
        kt = kp_ref[0, 0, pl.ds(start, tk), :]
        vt = vs_ref[0, 0, pl.ds(start, tk), :]
        sc = _nt_dot(q2, kt)
        if causal:
            pos = start + lax.broadcasted_iota(jnp.int32, (1, tk), 1)
            sc = jnp.where(pos <= t, sc, NEG)
        m_new = jnp.maximum(m, jnp.max(sc, axis=-1, keepdims=True))
        alpha = jnp.exp(m - m_new)
        p = jnp.exp(sc - m_new)
        l = alpha * l + jnp.sum(p, axis=-1, keepdims=True)
        acc = alpha * acc + jnp.dot(p.astype(BF16), vt, preferred_element_type=F32)
        return m_new, l, acc

    n_full = qs // tk
    init = (jnp.full((rows, 1), NEG, F32), jnp.zeros((rows, 1), F32), jnp.zeros((rows, DH_C), F32))
    carry = lax.fori_loop(0, n_full,
                          lambda i, c: flash_step(pl.multiple_of(i * tk, tk), c, False), init)
    _, l_s, acc_s = flash_step(pl.multiple_of(n_full * tk, tk), carry, True)
    o_s = acc_s / l_s

    wlen = WINDOW + tq
    wstart = pl.multiple_of(jnp.maximum(qs - WINDOW, 0), tq)
    kw = kw_ref[0, 0, pl.ds(wstart, wlen), :]
    vw = vw_ref[0, 0, pl.ds(wstart, wlen), :]
    sw = _nt_dot(q2[:, 0:DH_C], kw)
    dlt = t - (wstart + lax.broadcasted_iota(jnp.int32, (1, wlen), 1))
    okw = (dlt >= 0) & (dlt < WINDOW)
    sw = jnp.where(okw, sw, NEG)
    pw = jnp.where(okw, jnp.exp(sw - jnp.max(sw, axis=-1, keepdims=True)), 0.0)
    o_w = jnp.dot(pw.astype(BF16), vw, preferred_element_type=F32) / jnp.sum(pw, axis=-1, keepdims=True)

    gate = _sigmoid(gate_ref[0])
    for j in range(HPG_C):
        cs = slice(j * DH_C, (j + 1) * DH_C)
        rs = slice(j * tq, (j + 1) * tq)
        hc = (gate[:, j:j + 1] * oc_ref[0, :, cs] + gate[:, HPG_C + j:HPG_C + j + 1] * o_s[rs, :]
              + gate[:, 2 * HPG_C + j:2 * HPG_C + j + 1] * o_w[rs, :])
        y_ref[0, :, cs] = hc * _silu(z_ref[0, :, cs])


def _attn_call(proj, qa, kp, vs, kw, vw, oc):
    bsz, s, _ = proj.shape
    tq = TQ_ATT
    gw = HPG_C * DH_C

    def res_spec(w):
        return pl.BlockSpec((1, 1, s, w), lambda b_, g, i: (b_, g, 0, 0))

    return pl.pallas_call(
        _attn_kernel,
        grid=(bsz, G_C, s // tq),
        in_specs=[pl.BlockSpec((1, tq, 2 * gw), lambda b_, g, i: (b_, i, g)),
                  res_spec(2 * DH_C), res_spec(DH_C), res_spec(DH_C), res_spec(DH_C),
                  pl.BlockSpec((1, tq, gw), lambda b_, g, i: (b_, i, g)),
                  pl.BlockSpec((1, tq, LANE), lambda b_, g, i: (b_, i, COL_GATE // LANE + g)),
                  pl.BlockSpec((1, tq, gw), lambda b_, g, i: (b_, i, COL_CZ // gw + g))],
        out_specs=pl.BlockSpec((1, tq, gw), lambda b_, g, i: (b_, i, g)),
        out_shape=jax.ShapeDtypeStruct((bsz, s, W_MIX), F32),
        compiler_params=_cparams(("arbitrary", "arbitrary", "arbitrary")),
        name="nsa_attn",
    )(qa, kp, vs, kw, vw, oc, proj, proj)


def _merge_kernel(ya_ref, yb_ref, yc_ref, mg_ref, x_ref, mod_ref, g_ref, wb_ref, wo_ref, o_ref):
    merged = None
    for r, y_ref in enumerate((ya_ref, yb_ref, yc_ref)):
        br = jnp.dot(y_ref[0].astype(BF16), wb_ref[r], preferred_element_type=F32)
        term = _sigmoid(mg_ref[0, :, r * D_MODEL:(r + 1) * D_MODEL]) * br
        merged = term if merged is None else merged + term
    out = jnp.dot(merged.astype(BF16), wo_ref[...], preferred_element_type=F32)
    ms = jnp.mean(out * out, axis=-1, keepdims=True)
    o_ref[0] = x_ref[0] + mod_ref[0, 2:3, :] * (out * lax.rsqrt(ms + 1e-6) * g_ref[...])


def _merge_call(ya, yb, yc, proj, x, mod, g_post, wb, wo):
    bsz, s, _ = x.shape
    tm = TM_OUT
    row = pl.BlockSpec((1, tm, D_MODEL), lambda b_, i: (b_, i, 0))
    return pl.pallas_call(
        _merge_kernel,
        grid=(bsz, s // tm),
        in_specs=[row, row, row,
                  pl.BlockSpec((1, tm, N_BRANCH * D_MODEL), lambda b_, i: (b_, i, COL_MG // (N_BRANCH * D_MODEL))),
                  row,
                  pl.BlockSpec((1, 8, D_MODEL), lambda b_, i: (b_, 0, 0)),
                  pl.BlockSpec((1, D_MODEL), lambda b_, i: (0, 0)),
                  pl.BlockSpec((N_BRANCH, W_MIX, D_MODEL), lambda b_, i: (0, 0, 0)),
                  pl.BlockSpec((D_MODEL, D_MODEL), lambda b_, i: (0, 0))],
        out_specs=row,
        out_shape=jax.ShapeDtypeStruct((bsz, s, D_MODEL), F32),
        compiler_params=_cparams(("arbitrary", "arbitrary")),
        name="merge_out",
    )(ya, yb, yc, proj, x, mod, g_post, wb, wo)


def _proj_column_order():
    src = {}
    off = 0
    names = ("aq", "ak", "av", "ao", "az", "ai", "af", "ba", "bb", "bz", "cq", "kc", "vc", "ks", "vs",
             "kw", "vw", "cg", "cz", "mg")
    widths = (W_MIX,) * 5 + (H_A, H_A) + (W_MIX,) * 3 + (W_MIX,) + (KV_W,) * 6 + (3 * H_C, W_MIX,
                                                                                   N_BRANCH * D_MODEL)
    for nm, w in zip(names, widths):
        src[nm] = np.arange(off, off + w)
        off += w
    order = np.full((N_PROJ,), -1, np.int64)
    for nm, col in (("aq", COL_AQ), ("ak", COL_AK), ("av", COL_AV), ("ao", COL_AO), ("az", COL_AZ),
                    ("ba", COL_BA), ("bb", COL_BB), ("bz", COL_BZ), ("cq", COL_CQ), ("kc", COL_KC),
                    ("vc", COL_VC), ("ks", COL_KS), ("vs", COL_VS), ("kw", COL_KW), ("vw", COL_VW),
                    ("cz", COL_CZ), ("mg", COL_MG)):
        order[col:col + len(src[nm])] = src[nm]
    order[COL_SM:COL_SM + H_A] = src["ai"]
    order[COL_SM + H_A:COL_SM + 2 * H_A] = src["af"]
    for g in range(G_C):
        for r in range(3):
            for j in range(HPG_C):
                order[COL_GATE + g * LANE + r * HPG_C + j] = src["cg"][3 * (g * HPG_C + j) + r]
    return order


def _reorder_in_proj(w_in, b_in):
    order = _proj_column_order()
    idx = jnp.asarray(np.maximum(order, 0), jnp.int32)
    keep = jnp.asarray(order >= 0)
    w = jnp.where(keep, jnp.take(w_in, idx, axis=-1), 0.0)
    b = jnp.where(keep, jnp.take(b_in, idx, axis=-1), 0.0)
    return w, b


def kernel(x, c, w_ada, b_ada, norm_pre, norm_post, w_in, b_in, mlstm_conv_w, mlstm_conv_b, mlstm_norm,
           conf_dw_w, conf_dw_b, conf_ln_g, conf_ln_b, nsa_cmp_pe, nsa_cmp_w1, nsa_cmp_w2, w_branch, w_out):
    bsz, s, _ = x.shape
    depth = w_in.shape[0]
    tm = min(1024, s)

    c_pad = jnp.zeros((8, D_MODEL), F32).at[:bsz].set(c)
    mod_all = _ada_call(c_pad, w_ada, b_ada)
    w_r, b_r = _reorder_in_proj(w_in, b_in)
    w_bf = w_r.astype(BF16)
    wt = jnp.swapaxes(w_r[:, :, COL_SM:COL_SM + 8], 1, 2).astype(BF16)
    bt = b_r[:, COL_SM:COL_SM + 8, None]
    w1_bf = nsa_cmp_w1.astype(BF16)
    w2_bf = nsa_cmp_w2.astype(BF16)
    wb_bf = w_branch.astype(BF16)
    wo_bf = w_out.astype(BF16)

    for l in range(depth):
        m3 = mod_all[l, :bsz].reshape(bsz, 3, D_MODEL)
        mod = jnp.zeros((bsz, 8, D_MODEL), F32).at[:, :3].set(m3)
        proj, gt = _inproj_call(x, mod, norm_pre[l][None], w_bf[l], b_r[l][None], wt[l], bt[l], tm)
        ya = _mlstm_call(proj, gt, mlstm_conv_w[l], mlstm_conv_b[l][None], mlstm_norm[l][None])
        yb = _conf_call(proj, conf_dw_w[l], conf_dw_b[l][None], conf_ln_g[l][None], conf_ln_b[l][None])
        kvc = _compress_call(proj, nsa_cmp_pe[l], w1_bf[l], w2_bf[l])
        kp, vs, kw, vw = _kvprep_call(proj)
        qa, oc = _cmpsel_call(proj, kvc)
        yc = _attn_call(proj, qa, kp, vs, kw, vw, oc)
        x = _merge_call(ya, yb, yc, proj, x, mod, norm_post[l][None], wb_bf[l], wo_bf[l])
    return x
```

```python
import jax
import jax.numpy as jnp
import numpy as np
from jax import lax
from jax.experimental import pallas as pl
from jax.experimental.pallas import tpu as pltpu

F32 = jnp.float32
BF16 = jnp.bfloat16
HIGHEST = lax.Precision.HIGHEST

D_MODEL = 1024
W_MIX = 1024
N_BRANCH = 3
H_A = 4
DH_A = W_MIX // H_A
QK_CONV = 4
CONF_K = 31
H_C = 8
DH_C = W_MIX // H_C
G_C = 2
HPG_C = H_C // G_C
KV_W = G_C * DH_C
CMP_LEN = 32
CMP_STRIDE = 16
CMP_HIDDEN = 256
SLC_LEN = 64
SLC_SHIFT = 6
N_SLC = 16
WINDOW = 512
NEG = -1e30
LANE = 128
LOG2E = 1.4426950408889634

SRC_AQ, SRC_AI, SRC_BA, SRC_CQ = 0, 5120, 5128, 8200
SRC_KC, SRC_KS, SRC_VS, SRC_KW, SRC_VW, SRC_CG, SRC_CZ, SRC_MG = 9224, 9736, 9992, 10248, 10504, 10760, 10784, 11808
N_IN = 14880

COL_MG = 0
COL_AQ, COL_AK, COL_AV, COL_AO, COL_AZ = 3072, 4096, 5120, 6144, 7168
COL_BA, COL_BB, COL_BZ = 8192, 9216, 10240
COL_CZ = 11264
COL_KC, COL_VC, COL_KS, COL_KW = 12288, 12544, 12800, 13056
N_PROJ = 13312
ROW_Q, ROW_V, ROW_GT, ROW_NG = 0, 1024, 1536, 1544
NG_ROWS = 16
N_ROWS_T = ROW_NG + G_C * NG_ROWS

TN_PROJ = 3328
TM_PROJ = 1024
TM_PROJ_T = 512
L_MLSTM = 256
T_CONF = 256
CONF_HALO = 32
CONF_ROWS = 128
TQ_CMP = 512
TQ_ATT = 256
TK_ATT = 1024
TS_PREP = 1024
TM_OUT = 512
VMEM_LIMIT = 56 * 1024 * 1024


def _cparams(n_axes):
    return pltpu.CompilerParams(dimension_semantics=("arbitrary",) * n_axes, vmem_limit_bytes=VMEM_LIMIT)


def _sigmoid(x):
    return jax.nn.sigmoid(x)


def _silu(x):
    return x * jax.nn.sigmoid(x)


def _nt_dot(a, b):
    return lax.dot_general(a, b, (((1,), (1,)), ((), ())), preferred_element_type=F32)


def _dot(a, b):
    return jnp.dot(a, b, preferred_element_type=F32)


def _ada_kernel(c_ref, w_ref, b_ref, o_ref):
    o_ref[0] = jnp.dot(c_ref[...], w_ref[0], precision=HIGHEST, preferred_element_type=F32) + b_ref[0]


def _ada_call(c_pad, w_ada, b_ada):
    depth = w_ada.shape[0]
    return pl.pallas_call(
        _ada_kernel,
        grid=(depth, 3),
        in_specs=[pl.BlockSpec((8, D_MODEL), lambda l, j: (0, 0)),
                  pl.BlockSpec((1, D_MODEL, D_MODEL), lambda l, j: (l, 0, j)),
                  pl.BlockSpec((1, 1, D_MODEL), lambda l, j: (l, 0, j))],
        out_specs=pl.BlockSpec((1, 8, D_MODEL), lambda l, j: (l, 0, j)),
        out_shape=jax.ShapeDtypeStruct((depth, 8, 3 * D_MODEL), F32),
        compiler_params=_cparams(2),
        name="adaln",
    )(c_pad, w_ada, b_ada.reshape(depth, 1, 3 * D_MODEL))


def _modulated_norm(x, g_ref, mod_ref):
    ms = jnp.mean(x * x, axis=-1, keepdims=True)
    y = x * lax.rsqrt(ms + 1e-6) * g_ref[...]
    return (y * (1.0 + mod_ref[0, 1:2, :]) + mod_ref[0, 0:1, :]).astype(BF16)


def _inproj_kernel(x_ref, mod_ref, g_ref, w_ref, b_ref, proj_ref, h_scr):
    @pl.when(pl.program_id(2) == 0)
    def _():
        h_scr[...] = _modulated_norm(x_ref[0], g_ref, mod_ref)

    proj_ref[0] = (_dot(h_scr[...], w_ref[...]) + b_ref[...]).astype(BF16)


def _inproj_call(x, mod, g_pre, w, b):
    bsz, s, _ = x.shape
    tm = min(TM_PROJ, s)
    return pl.pallas_call(
        _inproj_kernel,
        grid=(bsz, s // tm, N_PROJ // TN_PROJ),
        in_specs=[pl.BlockSpec((1, tm, D_MODEL), lambda b_, i, j: (b_, i, 0)),
                  pl.BlockSpec((1, 8, D_MODEL), lambda b_, i, j: (b_, 0, 0)),
                  pl.BlockSpec((1, D_MODEL), lambda b_, i, j: (0, 0)),
                  pl.BlockSpec((D_MODEL, TN_PROJ), lambda b_, i, j: (0, j)),
                  pl.BlockSpec((1, TN_PROJ), lambda b_, i, j: (0, j))],
        out_specs=pl.BlockSpec((1, tm, TN_PROJ), lambda b_, i, j: (b_, i, j)),
        out_shape=jax.ShapeDtypeStruct((bsz, s, N_PROJ), BF16),
        scratch_shapes=[pltpu.VMEM((tm, D_MODEL), BF16)],
        compiler_params=_cparams(3),
        name="inproj",
    )(x, mod, g_pre, w, b)


def _inproj_t_kernel(x_ref, mod_ref, g_ref, wt_ref, bt_ref, wsm_ref, bsm_ref,
                     qt_ref, vt_ref, gt_ref, ng_ref, sm_ref):
    h = _modulated_norm(x_ref[0], g_ref, mod_ref)
    r = _nt_dot(wt_ref[...], h) + bt_ref[...]
    qt_ref[0] = (r[ROW_Q:ROW_V] * (DH_C ** -0.5 * LOG2E)).astype(BF16)
    vt_ref[0] = r[ROW_V:ROW_GT].astype(BF16)
    gt_ref[0] = r[ROW_GT:ROW_NG]
    ng_ref[0] = r[ROW_NG:N_ROWS_T]
    sm_ref[0] = _dot(h, wsm_ref[...]) + bsm_ref[...]


def _inproj_t_call(x, mod, g_pre, wt, bt, wsm, bsm):
    bsz, s, _ = x.shape
    tm = min(TM_PROJ_T, s)
    full = lambda shape: pl.BlockSpec(shape, lambda b_, i: (0,) * len(shape))
    return pl.pallas_call(
        _inproj_t_kernel,
        grid=(bsz, s // tm),
        in_specs=[pl.BlockSpec((1, tm, D_MODEL), lambda b_, i: (b_, i, 0)),
                  pl.BlockSpec((1, 8, D_MODEL), lambda b_, i: (b_, 0, 0)),
                  full((1, D_MODEL)), full((N_ROWS_T, D_MODEL)), full((N_ROWS_T, 1)),
                  full((D_MODEL, LANE)), full((1, LANE))],
        out_specs=[pl.BlockSpec((1, W_MIX, tm), lambda b_, i: (b_, 0, i)),
                   pl.BlockSpec((1, 2 * KV_W, tm), lambda b_, i: (b_, 0, i)),
                   pl.BlockSpec((1, 8, tm), lambda b_, i: (b_, 0, i)),
                   pl.BlockSpec((1, G_C * NG_ROWS, tm), lambda b_, i: (b_, 0, i)),
                   pl.BlockSpec((1, tm, LANE), lambda b_, i: (b_, i, 0))],
        out_shape=[jax.ShapeDtypeStruct((bsz, W_MIX, s), BF16),
                   jax.ShapeDtypeStruct((bsz, 2 * KV_W, s), BF16),
                   jax.ShapeDtypeStruct((bsz, 8, s), F32),
                   jax.ShapeDtypeStruct((bsz, G_C * NG_ROWS, s), F32),
                   jax.ShapeDtypeStruct((bsz, s, LANE), F32)],
        compiler_params=_cparams(2),
        name="inproj_t",
    )(x, mod, g_pre, wt, bt, wsm, bsm)


def _log_sigmoid(x):
    return jnp.minimum(x, 0.0) - jnp.log1p(jnp.exp(-jnp.abs(x)))


def _mlstm_kernel(q_ref, k_ref, v_ref, o_ref, z_ref, sm_ref, gt_ref, cw_ref, cb_ref, nrm_ref,
                  y_ref, xq_scr, xk_scr, c_scr, n_scr, m_scr):
    L = L_MLSTM

    @pl.when(pl.program_id(1) == 0)
    def _():
        xq_scr[0:8, :] = jnp.zeros((8, W_MIX), F32)
        xk_scr[0:8, :] = jnp.zeros((8, W_MIX), F32)
        c_scr[...] = jnp.zeros_like(c_scr)
        n_scr[...] = jnp.zeros_like(n_scr)
        m_scr[...] = jnp.zeros_like(m_scr)

    xq_scr[8:8 + L, :] = q_ref[0].astype(F32)
    xk_scr[8:8 + L, :] = k_ref[0].astype(F32)
    accq = jnp.broadcast_to(cb_ref[:, 0:W_MIX], (L, W_MIX))
    acck = jnp.broadcast_to(cb_ref[:, W_MIX:2 * W_MIX], (L, W_MIX))
    for j in range(QK_CONV):
        off = 8 - (QK_CONV - 1) + j
        accq = accq + cw_ref[j:j + 1, 0:W_MIX] * xq_scr[off:off + L, :]
        acck = acck + cw_ref[j:j + 1, W_MIX:2 * W_MIX] * xk_scr[off:off + L, :]
    xq_scr[0:8, :] = xq_scr[L:L + 8, :]
    xk_scr[0:8, :] = xk_scr[L:L + 8, :]
    qc = _silu(accq) * (DH_A ** -0.5)
    kc = _silu(acck)

    sm = sm_ref[0]
    row = lax.broadcasted_iota(jnp.int32, (L, L), 0)
    col = lax.broadcasted_iota(jnp.int32, (L, L), 1)
    causal = col <= row
    tril = causal.astype(F32)
    triu = (row <= col).astype(F32)
    bcum_c = jnp.dot(tril, _log_sigmoid(sm), precision=HIGHEST, preferred_element_type=F32)
    gt = gt_ref[0]
    bcum_r = jnp.dot(_log_sigmoid(gt), triu, precision=HIGHEST, preferred_element_type=F32)

    for h in range(H_A):
        cs = slice(h * DH_A, (h + 1) * DH_A)
        qh = qc[:, cs]
        kh = kc[:, cs]
        qb = qh.astype(BF16)
        kb = kh.astype(BF16)
        vb = v_ref[0, :, cs]
        i_c = sm[:, h:h + 1]
        b_c = bcum_c[:, H_A + h:H_A + h + 1]
        r_r = gt[h:h + 1, :] - bcum_r[H_A + h:H_A + h + 1, :]
        m_prev = m_scr[h][:, 0:1]
        c_prev = c_scr[h]
        n_prev = n_scr[h]

        dmat = jnp.where(causal, b_c + r_r, -jnp.inf)
        m_inter = b_c + m_prev
        m_t = jnp.maximum(m_inter, jnp.max(dmat, axis=-1, keepdims=True))
        w = jnp.exp(dmat - m_t)
        a_inter = jnp.exp(m_inter - m_t)
        sqk = _nt_dot(qb, kb) * w
        num = _dot(sqk.astype(BF16), vb) + a_inter * _dot(qb, c_prev.astype(BF16))
        den = jnp.sum(sqk, axis=-1, keepdims=True) + a_inter * jnp.sum(qh * n_prev, axis=-1, keepdims=True)
        hcell = num / jnp.maximum(jnp.abs(den), jnp.exp(-m_t))

        mu = jnp.mean(hcell, axis=-1, keepdims=True)
        hc0 = hcell - mu
        hn = hc0 * lax.rsqrt(jnp.mean(hc0 * hc0, axis=-1, keepdims=True) + 1e-5) * nrm_ref[:, cs]
        y_ref[0, :, cs] = (_sigmoid(o_ref[0, :, cs].astype(F32)) * hn
                           * _silu(z_ref[0, :, cs].astype(F32))).astype(BF16)

        b_last = b_c[L - 1:L, :]
        g = b_last - b_c + i_c
        m_new = jnp.maximum(b_last + m_prev, jnp.max(g, axis=0, keepdims=True))
        decay = jnp.exp(b_last + m_prev - m_new)
        wk = jnp.exp(g - m_new) * kh
        c_scr[h] = decay * c_prev + lax.dot_general(wk.astype(BF16), vb, (((0,), (0,)), ((), ())),
                                                    preferred_element_type=F32)
        n_scr[h] = decay * n_prev + jnp.sum(wk, axis=0, keepdims=True)
        m_scr[h] = jnp.broadcast_to(m_new, (1, LANE))


def _mlstm_call(proj, sm, gt, conv_w, conv_b, nrm):
    bsz, s, _ = proj.shape
    L = L_MLSTM

    def col_spec(col):
        cb = col // W_MIX
        return pl.BlockSpec((1, L, W_MIX), lambda b_, i: (b_, i, cb))

    return pl.pallas_call(
        _mlstm_kernel,
        grid=(bsz, s // L),
        in_specs=[col_spec(COL_AQ), col_spec(COL_AK), col_spec(COL_AV), col_spec(COL_AO), col_spec(COL_AZ),
                  pl.BlockSpec((1, L, LANE), lambda b_, i: (b_, i, 0)),
                  pl.BlockSpec((1, 8, L), lambda b_, i: (b_, 0, i)),
                  pl.BlockSpec((QK_CONV, 2 * W_MIX), lambda b_, i: (0, 0)),
                  pl.BlockSpec((1, 2 * W_MIX), lambda b_, i: (0, 0)),
                  pl.BlockSpec((1, W_MIX), lambda b_, i: (0, 0))],
        out_specs=pl.BlockSpec((1, L, W_MIX), lambda b_, i: (b_, i, 0)),
        out_shape=jax.ShapeDtypeStruct((bsz, s, W_MIX), BF16),
        scratch_shapes=[pltpu.VMEM((L + 8, W_MIX), F32), pltpu.VMEM((L + 8, W_MIX), F32),
                        pltpu.VMEM((H_A, DH_A, DH_A), F32), pltpu.VMEM((H_A, 1, DH_A), F32),
                        pltpu.VMEM((H_A, 1, LANE), F32)],
        compiler_params=_cparams(2),
        name="mlstm",
    )(proj, proj, proj, proj, proj, sm, gt, conv_w, conv_b, nrm)


def _conf_kernel(a_ref, b_ref, z_ref, w_ref, cb_ref, lg_ref, lb_ref, y_ref, g_scr, sh_scr, u_scr):
    T = T_CONF
    H = CONF_HALO

    @pl.when(pl.program_id(1) == 0)
    def _():
        g_scr[0:H, :] = jnp.zeros((H, W_MIX), F32)

    g_scr[H:H + T, :] = a_ref[0].astype(F32) * _sigmoid(b_ref[0].astype(F32))
    span = T + H - 8
    for r in range(1, 8):
        sh_scr[r - 1] = g_scr[r:r + span, :]
    tr = CONF_ROWS
    for c in range(W_MIX // LANE):
        cs = slice(c * LANE, (c + 1) * LANE)
        for r0 in range(0, T, tr):
            acc = jnp.broadcast_to(cb_ref[:, cs], (tr, LANE))
            for r in range(8):
                offs = [o for o in range(H - (CONF_K - 1), H + 1) if o % 8 == r]
                lo, hi = min(offs) - r, max(offs) - r + tr
                base = (g_scr[r0 + lo:r0 + hi, cs] if r == 0 else sh_scr[r - 1, r0 + lo:r0 + hi, cs])
                for o in offs:
                    j = o - (H - (CONF_K - 1))
                    acc = acc + w_ref[j:j + 1, cs] * base[o - r - lo:o - r - lo + tr, :]
            u_scr[r0:r0 + tr, cs] = acc
    g_scr[0:H, :] = g_scr[T:T + H, :]

    u = u_scr[...]
    mu = jnp.mean(u, axis=-1, keepdims=True)
    uc = u - mu
    var = jnp.mean(uc * uc, axis=-1, keepdims=True)
    ln = uc * lax.rsqrt(var + 1e-5) * lg_ref[...] + lb_ref[...]
    y_ref[0] = (_silu(ln) * _silu(z_ref[0].astype(F32))).astype(BF16)


def _conf_call(proj, dw_w, dw_b, ln_g, ln_b):
    bsz, s, _ = proj.shape
    T = T_CONF

    def col_spec(col):
        cb = col // W_MIX
        return pl.BlockSpec((1, T, W_MIX), lambda b_, i: (b_, i, cb))

    vec = pl.BlockSpec((1, W_MIX), lambda b_, i: (0, 0))
    return pl.pallas_call(
        _conf_kernel,
        grid=(bsz, s // T),
        in_specs=[col_spec(COL_BA), col_spec(COL_BB), col_spec(COL_BZ),
                  pl.BlockSpec((CONF_K, W_MIX), lambda b_, i: (0, 0)), vec, vec, vec],
        out_specs=pl.BlockSpec((1, T, W_MIX), lambda b_, i: (b_, i, 0)),
        out_shape=jax.ShapeDtypeStruct((bsz, s, W_MIX), BF16),
        scratch_shapes=[pltpu.VMEM((T + CONF_HALO, W_MIX), F32),
                        pltpu.VMEM((7, T + CONF_HALO - 8, W_MIX), F32),
                        pltpu.VMEM((T, W_MIX), F32)],
        compiler_params=_cparams(2),
        name="conformer",
    )(proj, proj, proj, dw_w, dw_b, ln_g, ln_b)


def _gelu_tanh(x):
    return 0.5 * x * (1.0 + jnp.tanh(np.sqrt(2.0 / np.pi) * (x + 0.044715 * (x * x * x))))


def _compress_kernel(x_ref, pe_ref, w1_ref, w2_ref, w2t_ref, o_ref, ot_ref, x_scr, b_scr):
    nb = o_ref.shape[3]
    x_scr[...] = x_ref[0].astype(F32)
    acc_a = jnp.zeros((nb, CMP_HIDDEN), F32)
    acc_b = jnp.zeros((nb, CMP_HIDDEN), F32)
    for p in range(CMP_STRIDE):
        xp = x_scr[pl.ds(p, nb, stride=CMP_STRIDE), :]
        xa = (xp + pe_ref[0, p:p + 1, :]).astype(BF16)
        xb = (xp + pe_ref[0, CMP_STRIDE + p:CMP_STRIDE + p + 1, :]).astype(BF16)
        acc_a = acc_a + _dot(xa, w1_ref[0, p * DH_C:(p + 1) * DH_C, :])
        acc_b = acc_b + _dot(xb, w1_ref[0, (CMP_STRIDE + p) * DH_C:(CMP_STRIDE + p + 1) * DH_C, :])
    b_scr[0:nb, :] = acc_b
    b_scr[nb:nb + 8, :] = jnp.zeros((8, CMP_HIDDEN), F32)
    hid = _gelu_tanh(acc_a + b_scr[1:nb + 1, :]).astype(BF16)
    o_ref[0, 0, 0] = _dot(hid, w2_ref[0]).astype(BF16)
    ot_ref[0, 0, 0] = _nt_dot(w2t_ref[0], hid).astype(BF16)


def _compress_call(proj, pe, w1, w2, w2t):
    bsz, s, _ = proj.shape
    nb = s // CMP_STRIDE
    return pl.pallas_call(
        _compress_kernel,
        grid=(bsz, 2, G_C),
        in_specs=[pl.BlockSpec((1, s, DH_C), lambda b_, j, g: (b_, 0, COL_KC // DH_C + 2 * j + g)),
                  pl.BlockSpec((1, CMP_LEN, DH_C), lambda b_, j, g: (j, 0, 0)),
                  pl.BlockSpec((1, CMP_LEN * DH_C, CMP_HIDDEN), lambda b_, j, g: (j, 0, 0)),
                  pl.BlockSpec((1, CMP_HIDDEN, DH_C), lambda b_, j, g: (j, 0, 0)),
                  pl.BlockSpec((1, DH_C, CMP_HIDDEN), lambda b_, j, g: (j, 0, 0))],
        out_specs=[pl.BlockSpec((1, 1, 1, nb, DH_C), lambda b_, j, g: (b_, j, g, 0, 0)),
                   pl.BlockSpec((1, 1, 1, DH_C, nb), lambda b_, j, g: (b_, j, g, 0, 0))],
        out_shape=[jax.ShapeDtypeStruct((bsz, 2, G_C, nb, DH_C), BF16),
                   jax.ShapeDtypeStruct((bsz, 2, G_C, DH_C, nb), BF16)],
        scratch_shapes=[pltpu.VMEM((s, DH_C), F32), pltpu.VMEM((nb + 8, CMP_HIDDEN), F32)],
        compiler_params=_cparams(3),
        name="nsa_compress",
    )(proj, pe, w1, w2, w2t)


def _kprep_kernel(ks_ref, kp_ref):
    ts = ks_ref.shape[1]
    pos = pl.program_id(2) * ts + lax.broadcasted_iota(jnp.int32, (ts, LANE), 0)
    lane = lax.broadcasted_iota(jnp.int32, (ts, LANE), 1)
    kp_ref[0, 0, :, 0:DH_C] = ks_ref[0]
    kp_ref[0, 0, :, DH_C:2 * DH_C] = jnp.where(lane == (pos >> SLC_SHIFT), 1.0, 0.0).astype(BF16)


def _kprep_call(proj):
    bsz, s, _ = proj.shape
    ts = min(TS_PREP, s)
    return pl.pallas_call(
        _kprep_kernel,
        grid=(bsz, G_C, s // ts),
        in_specs=[pl.BlockSpec((1, ts, DH_C), lambda b_, g, i: (b_, i, COL_KS // DH_C + g))],
        out_specs=pl.BlockSpec((1, 1, ts, 2 * DH_C), lambda b_, g, i: (b_, g, i, 0)),
        out_shape=jax.ShapeDtypeStruct((bsz, G_C, s, 2 * DH_C), BF16),
        compiler_params=_cparams(3),
        name="nsa_kprep",
    )(proj)


def _cmpsel_kernel(q_ref, kc_ref, vct_ref, qa_ref, oc_ref):
    tq = q_ref.shape[3]
    nb = kc_ref.shape[3]
    t = pl.program_id(2) * tq + lax.broadcasted_iota(jnp.int32, (1, tq), 1)
    n = lax.broadcasted_iota(jnp.int32, (nb, 1), 0)
    valid = (n * CMP_STRIDE + (CMP_LEN - 1) <= t) & (n < nb - 1)
    kc = kc_ref[0, 0, 0]
    vct = vct_ref[0, 0, 0]
    psum = jnp.zeros((nb, tq), F32)
    for j in range(HPG_C):
        qj = q_ref[0, j]
        qa_ref[0, j, 0:DH_C, :] = qj
        sc = jnp.where(valid, _dot(kc, qj), NEG)
        p = jnp.where(valid, jnp.exp2(sc - jnp.max(sc, axis=0, keepdims=True)), 0.0)
        p = p * (1.0 / jnp.maximum(jnp.sum(p, axis=0, keepdims=True), 1e-30))
        oc_ref[0, j] = _dot(vct, p.astype(BF16))
        psum = psum + p

    mm = lax.broadcasted_iota(jnp.int32, (LANE, nb), 0)
    nn = lax.broadcasted_iota(jnp.int32, (LANE, nb), 1)
    overlap = ((nn * CMP_STRIDE < mm * SLC_LEN + SLC_LEN) & (nn * CMP_STRIDE + CMP_LEN > mm * SLC_LEN)
               & (nn < nb - 1)).astype(F32)
    imp = jnp.dot(overlap, psum, precision=HIGHEST, preferred_element_type=F32)
    jb = lax.broadcasted_iota(jnp.int32, (LANE, tq), 0)
    cur = t >> SLC_SHIFT
    forced = (jb == 0) | (jb == cur) | (jb == cur - 1)
    imp = jnp.where(jb > cur, -1e4, jnp.where(forced, 1e4, imp))

    jf = jb.astype(F32)
    bias = jnp.full((LANE, tq), NEG, F32)
    for _ in range(N_SLC):
        mx = jnp.max(imp, axis=0, keepdims=True)
        idx = jnp.min(jnp.where(imp == mx, jf, float(LANE)), axis=0, keepdims=True)
        hit = jf == idx
        bias = jnp.where(hit, 0.0, bias)
        imp = jnp.where(hit, -jnp.inf, imp)
    bias = jnp.where(jb <= cur, bias, NEG).astype(BF16)
    for j in range(HPG_C):
        qa_ref[0, j, DH_C:2 * DH_C, :] = bias


def _cmpsel_call(qt, kvc, kvct):
    bsz, _, _, s = qt.shape
    nb = s // CMP_STRIDE
    tq = min(TQ_CMP, s)
    return pl.pallas_call(
        _cmpsel_kernel,
        grid=(bsz, G_C, s // tq),
        in_specs=[pl.BlockSpec((1, HPG_C, DH_C, tq), lambda b_, g, i: (b_, g, 0, i)),
                  pl.BlockSpec((1, 1, 1, nb, DH_C), lambda b_, g, i: (b_, 0, g, 0, 0)),
                  pl.BlockSpec((1, 1, 1, DH_C, nb), lambda b_, g, i: (b_, 1, g, 0, 0))],
        out_specs=[pl.BlockSpec((1, HPG_C, 2 * DH_C, tq), lambda b_, g, i: (b_, g, 0, i)),
                   pl.BlockSpec((1, HPG_C, DH_C, tq), lambda b_, g, i: (b_, g, 0, i))],
        out_shape=[jax.ShapeDtypeStruct((bsz, H_C, 2 * DH_C, s), BF16),
                   jax.ShapeDtypeStruct((bsz, H_C, DH_C, s), F32)],
        compiler_params=_cparams(3),
        name="nsa_cmpsel",
    )(qt, kvc, kvct)


def _attn_kernel(qa_ref, kp_ref, vs_ref, kw_ref, vw_ref, oc_ref, gate_ref, z_ref, y_ref):
    tq = qa_ref.shape[3]
    s_len = kp_ref.shape[2]
    tk = min(TK_ATT, s_len)
    rows = HPG_C * tq
    qs = pl.program_id(2) * tq
    q2 = jnp.concatenate([qa_ref[0, j] for j in range(HPG_C)], axis=1)
    t = qs + (lax.broadcasted_iota(jnp.int32, (1, rows), 1) & (tq - 1))

    def flash_step(start, carry, causal):
        m, l, acc = carry
        sc = _dot(kp_ref[0, 0, pl.ds(start, tk), :], q2)
        if causal:
            pos = start + lax.broadcasted_iota(jnp.int32, (tk, 1), 0)
            sc = jnp.where(pos <= t, sc, NEG)
        m_new = jnp.maximum(m, jnp.max(sc, axis=0, keepdims=True))
        alpha = jnp.exp2(m - m_new)
        p = jnp.exp2(sc - m_new)
        l = alpha * l + jnp.sum(p, axis=0, keepdims=True)
        acc = alpha * acc + _dot(vs_ref[0, 0, :, pl.ds(start, tk)], p.astype(BF16))
        return m_new, l, acc

    n_full = qs // tk
    init = (jnp.full((1, rows), NEG, F32), jnp.zeros((1, rows), F32), jnp.zeros((DH_C, rows), F32))
    carry = lax.fori_loop(0, n_full, lambda i, c: flash_step(pl.multiple_of(i * tk, tk), c, False), init)
    _, l_s, acc_s = flash_step(pl.multiple_of(n_full * tk, tk), carry, True)
    o_s = acc_s * (1.0 / l_s)

    wlen = min(WINDOW + tq, s_len)
    wstart = pl.multiple_of(jnp.maximum(qs + tq - wlen, 0), tq)
    sw = _dot(kw_ref[0, pl.ds(wstart, wlen), :], q2[0:DH_C, :])
    dlt = t - (wstart + lax.broadcasted_iota(jnp.int32, (wlen, 1), 0))
    okw = (dlt >= 0) & (dlt < WINDOW)
    sw = jnp.where(okw, sw, NEG)
    pw = jnp.where(okw, jnp.exp2(sw - jnp.max(sw, axis=0, keepdims=True)), 0.0)
    o_w = (_dot(vw_ref[0, 0, :, pl.ds(wstart, wlen)], pw.astype(BF16))
           * (1.0 / jnp.sum(pw, axis=0, keepdims=True)))

    gate = _sigmoid(gate_ref[0, 0])
    for j in range(HPG_C):
        cs = slice(j * DH_C, (j + 1) * DH_C)
        rs = slice(j * tq, (j + 1) * tq)
        hct = (gate[j:j + 1, :] * oc_ref[0, j] + gate[HPG_C + j:HPG_C + j + 1, :] * o_s[:, rs]
               + gate[2 * HPG_C + j:2 * HPG_C + j + 1, :] * o_w[:, rs])
        y_ref[0, :, cs] = (hct.T * _silu(z_ref[0, :, cs].astype(F32))).astype(BF16)


def _attn_call(proj, qa, kp, vt, oc, ng):
    bsz, s, _ = proj.shape
    tq = min(TQ_ATT, s)
    gw = HPG_C * DH_C
    return pl.pallas_call(
        _attn_kernel,
        grid=(bsz, G_C, s // tq),
        in_specs=[pl.BlockSpec((1, HPG_C, 2 * DH_C, tq), lambda b_, g, i: (b_, g, 0, i)),
                  pl.BlockSpec((1, 1, s, 2 * DH_C), lambda b_, g, i: (b_, g, 0, 0)),
                  pl.BlockSpec((1, 1, DH_C, s), lambda b_, g, i: (b_, g, 0, 0)),
                  pl.BlockSpec((1, s, DH_C), lambda b_, g, i: (b_, 0, COL_KW // DH_C + g)),
                  pl.BlockSpec((1, 1, DH_C, s), lambda b_, g, i: (b_, G_C + g, 0, 0)),
                  pl.BlockSpec((1, HPG_C, DH_C, tq), lambda b_, g, i: (b_, g, 0, i)),
                  pl.BlockSpec((1, 1, NG_ROWS, tq), lambda b_, g, i: (b_, g, 0, i)),
                  pl.BlockSpec((1, tq, gw), lambda b_, g, i: (b_, i, COL_CZ // gw + g))],
        out_specs=pl.BlockSpec((1, tq, gw), lambda b_, g, i: (b_, i, g)),
        out_shape=jax.ShapeDtypeStruct((bsz, s, W_MIX), BF16),
        compiler_params=_cparams(3),
        name="nsa_attn",
    )(qa, kp, vt, proj, vt, oc, ng, proj)


def _merge_kernel(ya_ref, yb_ref, yc_ref, mg_ref, x_ref, mod_ref, g_ref, wb_ref, wo_ref, o_ref):
    merged = None
    for r, y_ref in enumerate((ya_ref, yb_ref, yc_ref)):
        term = _sigmoid(mg_ref[0, :, r * D_MODEL:(r + 1) * D_MODEL].astype(F32)) * _dot(y_ref[0], wb_ref[r])
        merged = term if merged is None else merged + term
    out = _dot(merged.astype(BF16), wo_ref[...])
    ms = jnp.mean(out * out, axis=-1, keepdims=True)
    o_ref[0] = x_ref[0] + mod_ref[0, 2:3, :] * (out * lax.rsqrt(ms + 1e-6) * g_ref[...])


def _merge_call(ya, yb, yc, proj, x, mod, g_post, wb, wo):
    bsz, s, _ = x.shape
    tm = TM_OUT
    row = pl.BlockSpec((1, tm, D_MODEL), lambda b_, i: (b_, i, 0))
    return pl.pallas_call(
        _merge_kernel,
        grid=(bsz, s // tm),
        in_specs=[row, row, row,
                  pl.BlockSpec((1, tm, N_BRANCH * D_MODEL), lambda b_, i: (b_, i, COL_MG // (N_BRANCH * D_MODEL))),
                  row,
                  pl.BlockSpec((1, 8, D_MODEL), lambda b_, i: (b_, 0, 0)),
                  pl.BlockSpec((1, D_MODEL), lambda b_, i: (0, 0)),
                  pl.BlockSpec((N_BRANCH, W_MIX, D_MODEL), lambda b_, i: (0, 0, 0)),
                  pl.BlockSpec((D_MODEL, D_MODEL), lambda b_, i: (0, 0))],
        out_specs=row,
        out_shape=jax.ShapeDtypeStruct((bsz, s, D_MODEL), F32),
        compiler_params=_cparams(2),
        name="merge_out",
    )(ya, yb, yc, proj, x, mod, g_post, wb, wo)


def _cols(t, lo, hi):
    return t[..., lo:hi]


def _layout_params(w_in, b_in):
    def both(fn):
        return fn(w_in), fn(b_in)

    main = lambda t: jnp.concatenate(
        [_cols(t, SRC_MG, N_IN), _cols(t, SRC_AQ, SRC_AI), _cols(t, SRC_BA, SRC_CQ), _cols(t, SRC_CZ, SRC_MG),
         _cols(t, SRC_KC, SRC_VS), _cols(t, SRC_KW, SRC_VW)], axis=-1)
    gate_idx = np.array([[3 * (g * HPG_C + j) + r for r in range(3) for j in range(HPG_C)] for g in range(G_C)])

    def gates(t):
        cg = _cols(t, SRC_CG, SRC_CZ)
        parts = []
        for g in range(G_C):
            parts += [jnp.take(cg, jnp.asarray(gate_idx[g]), axis=-1),
                      jnp.zeros(cg.shape[:-1] + (NG_ROWS - 3 * HPG_C,), cg.dtype)]
        return jnp.concatenate(parts, axis=-1)

    trans = lambda t: jnp.concatenate(
        [_cols(t, SRC_CQ, SRC_KC), _cols(t, SRC_VS, SRC_KW), _cols(t, SRC_VW, SRC_CG),
         _cols(t, SRC_AI, SRC_BA), gates(t)], axis=-1)
    small = lambda t: jnp.concatenate(
        [_cols(t, SRC_AI, SRC_BA), jnp.zeros(t.shape[:-1] + (LANE - 2 * H_A,), t.dtype)], axis=-1)
    return both(main), both(trans), both(small)


def kernel(x, c, w_ada, b_ada, norm_pre, norm_post, w_in, b_in, mlstm_conv_w, mlstm_conv_b, mlstm_norm,
           conf_dw_w, conf_dw_b, conf_ln_g, conf_ln_b, nsa_cmp_pe, nsa_cmp_w1, nsa_cmp_w2, w_branch, w_out):
    bsz, s, _ = x.shape
    depth = w_in.shape[0]

    c_pad = jnp.zeros((8, D_MODEL), F32).at[:bsz].set(c)
    mod_all = _ada_call(c_pad, w_ada, b_ada)
    (w_m, b_m), (w_t, b_t), (w_s, b_s) = _layout_params(w_in, b_in)
    w_m = w_m.astype(BF16)
    w_t = jnp.swapaxes(w_t, 1, 2).astype(BF16)
    w_s = w_s.astype(BF16)
    w1_bf = nsa_cmp_w1.astype(BF16)
    w2_bf = nsa_cmp_w2.astype(BF16)
    w2t_bf = jnp.swapaxes(nsa_cmp_w2, 2, 3).astype(BF16)
    wb_bf = w_branch.astype(BF16)
    wo_bf = w_out.astype(BF16)

    for l in range(depth):
        m3 = mod_all[l, :bsz].reshape(bsz, 3, D_MODEL)
        mod = jnp.zeros((bsz, 8, D_MODEL), F32).at[:, :3].set(m3)
        g_pre = norm_pre[l][None]
        proj = _inproj_call(x, mod, g_pre, w_m[l], b_m[l][None])
        qt, vt, gt, ng, sm = _inproj_t_call(x, mod, g_pre, w_t[l], b_t[l][:, None], w_s[l], b_s[l][None])
        ya = _mlstm_call(proj, sm, gt, mlstm_conv_w[l], mlstm_conv_b[l][None], mlstm_norm[l][None])
        yb = _conf_call(proj, conf_dw_w[l], conf_dw_b[l][None], conf_ln_g[l][None], conf_ln_b[l][None])
        kvc, kvct = _compress_call(proj, nsa_cmp_pe[l], w1_bf[l], w2_bf[l], w2t_bf[l])
        kp = _kprep_call(proj)
        qa, oc = _cmpsel_call(qt.reshape(bsz, H_C, DH_C, s), kvc, kvct)
        yc = _attn_call(proj, qa, kp, vt.reshape(bsz, 2 * G_C, DH_C, s), oc,
                        ng.reshape(bsz, G_C, NG_ROWS, s))
        x = _merge_call(ya, yb, yc, proj, x, mod, norm_post[l][None], wb_bf[l], wo_bf[l])
    return x
```
